```python
import math
import jax, jax.numpy as jnp
from jax import lax
import numpy as np

D_MODEL = 1024
BATCH = 32
SEQ = 256
DEPTH = 1
DEC_BATCH = 2
DEC_SEQ = 1024
PAST_LEN = 512

GRID_W = 64
MIX_WIDTH = D_MODEL
H_A = 4
DK_A = MIX_WIDTH // 8
DV_A = MIX_WIDTH // 8
H_B = 4
D_HEAD_B = MIX_WIDTH // 16
D_FF = ((8 * D_MODEL // 3 + 127) // 128) * 128
HGRN_CHUNK = 32
Q_BLOCK = 128
ROPE_THETA = 10000.0
N_MOD = 9
EPS = 1e-6

kernel_name = "hymba_hgrn2_diffattn_macaron_dit_step"


def rms_norm(x, g):
    xf = x.astype(jnp.float32)
    y = xf * lax.rsqrt(jnp.mean(xf * xf, axis=-1, keepdims=True) + EPS)
    return (y * g.astype(jnp.float32)).astype(x.dtype)


def swiglu(h, w_in, w_out):
    gate, up = jnp.split(h @ w_in, 2, axis=-1)
    return (jax.nn.silu(gate) * up) @ w_out


def rope_tables(n_tokens):
    rows = n_tokens // GRID_W
    pos_row = jnp.repeat(jnp.arange(rows, dtype=jnp.float32), GRID_W)
    pos_col = jnp.tile(jnp.arange(GRID_W, dtype=jnp.float32), rows)
    half = D_HEAD_B // 2
    inv = ROPE_THETA ** (-jnp.arange(0, half, 2, dtype=jnp.float32) / half)
    ang_r = pos_row[:, None] * inv
    ang_c = pos_col[:, None] * inv
    return (jnp.cos(ang_r), jnp.sin(ang_r), jnp.cos(ang_c), jnp.sin(ang_c))


def apply_rope_2d(x, tables):
    cr, sr, cc, sc = [t[None, :, None, None, :].astype(x.dtype) for t in tables]
    xr, xc = jnp.split(x, 2, axis=-1)

    def rot(a, cos, sin):
        a1, a2 = jnp.split(a, 2, axis=-1)
        return jnp.concatenate([a1 * cos - a2 * sin, a2 * cos + a1 * sin], axis=-1)

    return jnp.concatenate([rot(xr, cr, sr), rot(xc, cc, sc)], axis=-1)


def hgrn_scan(q, k, log_f, v, s0):
    B, T, H, _ = q.shape
    dv = v.shape[-1]
    n = T // HGRN_CHUNK

    def chunks(a):
        return a.astype(jnp.float32).reshape(B, n, HGRN_CHUNK, H, a.shape[-1]).swapaxes(0, 1)

    tri = jnp.tril(jnp.ones((HGRN_CHUNK, HGRN_CHUNK), dtype=bool))[None, :, :, None, None]

    def body(s, blk):
        qc, kc, lf, vc = blk
        b = jnp.cumsum(lf, axis=1)
        decay = jnp.exp(jnp.where(tri, b[:, :, None] - b[:, None, :], -jnp.inf))
        scores = jnp.einsum('bthk,bshk,btshk->bhts', qc, kc, decay)
        o = (jnp.einsum('bhts,bshv->bthv', scores, vc)
             + jnp.einsum('bthk,bhkv->bthv', qc * jnp.exp(b), s))
        b_end = b[:, -1]
        s = (jnp.exp(b_end)[..., None] * s
             + jnp.einsum('bshk,bshv->bhkv', kc * jnp.exp(b_end[:, None] - b), vc))
        return s, o

    s_fin, o = lax.scan(body, s0.astype(jnp.float32), (chunks(q), chunks(k), chunks(log_f), chunks(v)))
    return o.swapaxes(0, 1).reshape(B, T, H, dv), s_fin


def diff_attention(q, k, v, lam):
    B, Tq = q.shape[:2]
    scale = D_HEAD_B ** -0.5
    qb = q.reshape(B, Tq // Q_BLOCK, Q_BLOCK, H_B, 2, D_HEAD_B).swapaxes(0, 1)

    def one_block(qi):
        s = jnp.einsum('bqhmd,bkhmd->bhmqk', qi, k).astype(jnp.float32) * scale
        p = jax.nn.softmax(s, axis=-1)
        a = p[:, :, 0] - lam * p[:, :, 1]
        return jnp.einsum('bhqk,bkhe->bqhe', a.astype(v.dtype), v)

    o = lax.map(one_block, qb)
    return o.swapaxes(0, 1).reshape(B, Tq, H_B, 2 * D_HEAD_B)


def token_mix(h, p, lb, lam_init, ctx_cache=None, rope=None):
    B, T, _ = h.shape
    sizes = [H_A * DK_A, H_A * DV_A, H_A * DK_A, H_A * DK_A, H_A * DV_A,
             H_B * 2 * D_HEAD_B, H_B * 2 * D_HEAD_B, H_B * 2 * D_HEAD_B]
    idx = []
    acc = 0
    for s in sizes[:-1]:
        acc += s
        idx.append(acc)
    proj = h @ p['w_mix_in']
    hq, hi, hf_fwd, hf_bwd, hg, aq, ak, av = jnp.split(proj, idx, axis=-1)

    def heads(a, d):
        return a.reshape(B, T, H_A, d)

    q_rec = heads(hq, DK_A)
    i_rec = heads(hi, DV_A)
    lb_h = lb.reshape(H_A, DK_A)

    def forget(hf):
        f = lb_h + (1.0 - lb_h) * jax.nn.sigmoid(heads(hf, DK_A).astype(jnp.float32))
        return jnp.log(f), 1.0 - f

    lf_f, k_f = forget(hf_fwd)
    lf_b, k_b = forget(hf_bwd)
    if ctx_cache is None:
        s0_f = jnp.zeros((B, H_A, DK_A, DV_A), jnp.float32)
        s0_b = s0_f
    else:
        s0_f = ctx_cache[2][:, 0]
        s0_b = ctx_cache[2][:, 1]
    o_f, s_f = hgrn_scan(q_rec, k_f, lf_f, i_rec, s0_f)
    rev = lambda a: jnp.flip(a, axis=1)
    o_b, s_b = hgrn_scan(rev(q_rec), rev(k_b), rev(lf_b), rev(i_rec), s0_b)
    o_rec = rms_norm(o_f + rev(o_b), p['hgrn_out_norm']).astype(h.dtype) * jax.nn.silu(heads(hg, DV_A))

    q_att = rms_norm(aq.reshape(B, T, H_B, 2, D_HEAD_B), p['attn_q_norm'])
    k_att = rms_norm(ak.reshape(B, T, H_B, 2, D_HEAD_B), p['attn_k_norm'])
    v_att = av.reshape(B, T, H_B, 2 * D_HEAD_B)
    if ctx_cache is None:
        keys, vals = k_att, v_att
    else:
        q_att = apply_rope_2d(q_att, rope)
        k_lat = apply_rope_2d(k_att, rope)
        keys = jnp.concatenate([ctx_cache[0].astype(h.dtype), k_lat], axis=1)
        vals = jnp.concatenate([ctx_cache[1].astype(h.dtype), v_att], axis=1)
    lq1, lk1, lq2, lk2 = p['attn_lambda'].astype(jnp.float32)
    lam = jnp.exp(jnp.sum(lq1 * lk1)) - jnp.exp(jnp.sum(lq2 * lk2)) + lam_init
    o_att = diff_attention(q_att, keys, vals, lam)
    o_att = rms_norm(o_att, p['attn_subln']) * (1.0 - lam_init)

    o = jnp.concatenate([o_rec.reshape(B, T, -1), o_att.reshape(B, T, -1)], axis=-1) @ p['w_mix_out']
    return o, k_att, v_att, jnp.stack([s_f, s_b], axis=1)


def layer(x, cond, p, lb, lam_init, ctx_cache=None, rope=None):
    m = (jax.nn.silu(cond) @ p['w_ada'] + p['b_ada']).reshape(cond.shape[0], N_MOD, D_MODEL)[:, :, None, :]
    sh1, sc1, g1, sh2, sc2, g2, sh3, sc3, g3 = [m[:, j] for j in range(N_MOD)]
    h = rms_norm(x, p['norm_ffn1']) * (1.0 + sc1) + sh1
    x = x + 0.5 * g1 * swiglu(h, p['w_ffn1_in'], p['w_ffn1_out'])
    h = rms_norm(x, p['norm_mix']) * (1.0 + sc2) + sh2
    mix, k_ctx, v_ctx, s_ctx = token_mix(h, p, lb, lam_init, ctx_cache, rope)
    x = x + g2 * mix
    h = rms_norm(x, p['norm_ffn2']) * (1.0 + sc3) + sh3
    x = x + 0.5 * g3 * swiglu(h, p['w_ffn2_in'], p['w_ffn2_out'])
    return x, k_ctx, v_ctx, s_ctx


def setup_inputs(seed: int = 0) -> dict:
    key = jax.random.key(seed)
    ks = jax.random.split(key, 24)
    f32 = jnp.float32

    def nrm(k, shape, s=1.0):
        return s * jax.random.normal(k, shape, f32)

    mix_in = 3 * H_A * DK_A + 2 * H_A * DV_A + 3 * H_B * 2 * D_HEAD_B
    mix_out = H_A * DV_A + H_B * 2 * D_HEAD_B
    return {
        "x_prompt": nrm(ks[0], (BATCH, SEQ, D_MODEL)),
        "x_sample": nrm(ks[1], (DEC_BATCH, DEC_SEQ, D_MODEL)),
        "c": nrm(ks[2], (DEC_BATCH, D_MODEL)),
        "cache_attn_k": nrm(ks[3], (DEC_BATCH, DEPTH, PAST_LEN, H_B, 2, D_HEAD_B)),
        "cache_attn_v": nrm(ks[4], (DEC_BATCH, DEPTH, PAST_LEN, H_B, 2 * D_HEAD_B), 0.5),
        "state_hgrn": nrm(ks[5], (DEC_BATCH, DEPTH, 2, H_A, DK_A, DV_A), 0.5),
        "c_ctx": nrm(ks[6], (D_MODEL,)),
        "w_ada": nrm(ks[7], (DEPTH, D_MODEL, N_MOD * D_MODEL), 0.5 * D_MODEL ** -0.5),
        "b_ada": nrm(ks[8], (DEPTH, N_MOD * D_MODEL), 0.02),
        "norm_ffn1": 1.0 + nrm(ks[9], (DEPTH, D_MODEL), 0.05),
        "w_ffn1_in": nrm(ks[10], (DEPTH, D_MODEL, 2 * D_FF), D_MODEL ** -0.5),
        "w_ffn1_out": nrm(ks[11], (DEPTH, D_FF, D_MODEL), D_FF ** -0.5),
        "norm_mix": 1.0 + nrm(ks[12], (DEPTH, D_MODEL), 0.05),
        "w_mix_in": nrm(ks[13], (DEPTH, D_MODEL, mix_in), D_MODEL ** -0.5),
        "w_mix_out": nrm(ks[14], (DEPTH, mix_out, D_MODEL), mix_out ** -0.5),
        "hgrn_lb_logits": nrm(ks[15], (DEPTH + 1, H_A * DK_A), 0.5),
        "hgrn_out_norm": 1.0 + nrm(ks[16], (DEPTH, DV_A), 0.05),
        "attn_q_norm": 1.0 + nrm(ks[17], (DEPTH, D_HEAD_B), 0.05),
        "attn_k_norm": 1.0 + nrm(ks[18], (DEPTH, D_HEAD_B), 0.05),
        "attn_lambda": nrm(ks[19], (DEPTH, 4, D_HEAD_B), 0.1),
        "attn_subln": 1.0 + nrm(ks[20], (DEPTH, 2 * D_HEAD_B), 0.05),
        "norm_ffn2": 1.0 + nrm(ks[21], (DEPTH, D_MODEL), 0.05),
        "w_ffn2_in": nrm(ks[22], (DEPTH, D_MODEL, 2 * D_FF), D_MODEL ** -0.5),
        "w_ffn2_out": nrm(ks[23], (DEPTH, D_FF, D_MODEL), D_FF ** -0.5),
    }


def reference(x_prompt, x_sample, c, cache_attn_k, cache_attn_v, state_hgrn, c_ctx,
              w_ada, b_ada, norm_ffn1, w_ffn1_in, w_ffn1_out, norm_mix, w_mix_in, w_mix_out,
              hgrn_lb_logits, hgrn_out_norm, attn_q_norm, attn_k_norm, attn_lambda, attn_subln,
              norm_ffn2, w_ffn2_in, w_ffn2_out):
    lb_all = jnp.cumsum(jax.nn.softmax(hgrn_lb_logits.astype(jnp.float32), axis=0), axis=0)
    rope = rope_tables(x_sample.shape[1])
    y_prompt = x_prompt
    y_sample = x_sample
    ks_list, vs_list, ss_list = [], [], []
    for l in range(DEPTH):
        p = {
            'w_ada': w_ada[l], 'b_ada': b_ada[l],
            'norm_ffn1': norm_ffn1[l], 'w_ffn1_in': w_ffn1_in[l], 'w_ffn1_out': w_ffn1_out[l],
            'norm_mix': norm_mix[l], 'w_mix_in': w_mix_in[l], 'w_mix_out': w_mix_out[l],
            'hgrn_out_norm': hgrn_out_norm[l], 'attn_q_norm': attn_q_norm[l],
            'attn_k_norm': attn_k_norm[l], 'attn_lambda': attn_lambda[l], 'attn_subln': attn_subln[l],
            'norm_ffn2': norm_ffn2[l], 'w_ffn2_in': w_ffn2_in[l], 'w_ffn2_out': w_ffn2_out[l],
        }
        lam_init = 0.8 - 0.6 * math.exp(-0.3 * l)
        y_prompt, k_l, v_l, s_l = layer(y_prompt, c_ctx[None], p, lb_all[l], lam_init)
        y_sample, _, _, _ = layer(y_sample, c, p, lb_all[l], lam_init,
                                  (cache_attn_k[:, l], cache_attn_v[:, l], state_hgrn[:, l]), rope)
        ks_list.append(k_l)
        vs_list.append(v_l)
        ss_list.append(s_l)
    new_attn_k = jnp.stack(ks_list, axis=1)
    new_attn_v = jnp.stack(vs_list, axis=1)
    new_state = jnp.stack(ss_list, axis=1)
    return (y_prompt, y_sample, new_attn_k, new_attn_v, new_state)
```

```python
import functools
import math

import jax
import jax.numpy as jnp
from jax import lax
from jax.experimental import pallas as pl
from jax.experimental.pallas import tpu as pltpu

D_MODEL = 1024
GRID_W = 64
H_A = 4
DK_A = 128
DV_A = 128
H_B = 4
D_HEAD_B = 64
D_FF = 2816
ROPE_THETA = 10000.0
N_MOD = 9
EPS = 1e-6
LAM_INIT = 0.8 - 0.6 * math.exp(-0.3 * 0)

MIX_A = H_A * DK_A
MIX_B = H_B * 2 * D_HEAD_B
CHUNK = 32
SUBLANES = 8
TOKEN_TILE = 512
FF_SPLITS = 2
ATTN_Q_BLOCK = 256
VMEM_LIMIT = 56 * 1024 * 1024

F32 = jnp.float32
BF16 = jnp.bfloat16
LOG2E = 1.0 / math.log(2.0)


def _dot(a, b):
    return jnp.dot(a, b, preferred_element_type=F32)


def _dot_nt(a, b):
    return lax.dot_general(a, b, (((1,), (1,)), ((), ())), preferred_element_type=F32)


def _dot_tn(a, b):
    return lax.dot_general(a, b, (((0,), (0,)), ((), ())), preferred_element_type=F32)


def _silu(x):
    return x * jax.nn.sigmoid(x)


def _mod_norm(x, gain, scale, shift):
    ms = jnp.mean(x * x, axis=-1, keepdims=True)
    return (x * lax.rsqrt(ms + EPS) * gain) * (1.0 + scale) + shift


def _split_bf16(x):
    hi = x.astype(BF16)
    lo = (x - hi.astype(F32)).astype(BF16)
    return hi, lo


def _ada_kernel(c_ref, w_ref, b_ref, o_ref):
    a = _silu(c_ref[...])
    rows = a.shape[0]
    a_hi = a.astype(BF16).astype(F32)
    lhs = jnp.concatenate([a_hi, a - a_hi], axis=0).astype(BF16)
    w_hi, w_lo = _split_bf16(w_ref[...])
    r = _dot(lhs, w_hi) + _dot(lhs, w_lo)
    o_ref[...] = r[:rows] + r[rows:] + b_ref[...]


def _ada_modulation(cond, w_ada, b_ada):
    rows = cond.shape[0]
    n = N_MOD * D_MODEL
    tn = n // 8
    out = pl.pallas_call(
        _ada_kernel,
        out_shape=jax.ShapeDtypeStruct((rows, n), F32),
        grid=(n // tn,),
        in_specs=[pl.BlockSpec((rows, D_MODEL), lambda j: (0, 0)),
                  pl.BlockSpec((D_MODEL, tn), lambda j: (0, j)),
                  pl.BlockSpec((1, tn), lambda j: (0, j))],
        out_specs=pl.BlockSpec((rows, tn), lambda j: (0, j)),
        compiler_params=pltpu.CompilerParams(dimension_semantics=("arbitrary",), vmem_limit_bytes=VMEM_LIMIT),
        name="ada_modulation",
    )(cond, w_ada, b_ada.reshape(1, n))
    return out.reshape(rows, N_MOD, D_MODEL)


def _ffn(x, mod_ref, j, gain_ref, win_ref, wout_ref):
    shift, scale, gate_mod = mod_ref[j:j + 1, :], mod_ref[j + 1:j + 2, :], mod_ref[j + 2:j + 3, :]
    hb = _mod_norm(x, gain_ref[...], scale, shift).astype(BF16)
    tf = D_FF // FF_SPLITS
    acc = None
    for c in range(FF_SPLITS):
        gate = _dot(hb, win_ref[:, c * tf:(c + 1) * tf])
        up = _dot(hb, win_ref[:, D_FF + c * tf:D_FF + (c + 1) * tf])
        part = _dot((_silu(gate) * up).astype(BF16), wout_ref[c * tf:(c + 1) * tf, :])
        acc = part if acc is None else acc + part
    return x + (0.5 * gate_mod) * acc


def _ffn1_kernel(x_ref, mod_ref, gain_ref, win_ref, wout_ref, o_ref):
    o_ref[...] = _ffn(x_ref[...], mod_ref, 0, gain_ref, win_ref, wout_ref)


def _mix_ffn2_kernel(x_ref, orec_ref, oatt_ref, mod_ref, wmo_ref, gain_ref, win_ref, wout_ref, o_ref):
    mix = _dot(orec_ref[...], wmo_ref[:MIX_A, :]) + _dot(oatt_ref[...], wmo_ref[MIX_A:, :])
    x = x_ref[...] + mod_ref[5:6, :] * mix
    o_ref[...] = _ffn(x, mod_ref, 6, gain_ref, win_ref, wout_ref)


def _resident(shape):
    return pl.BlockSpec(shape, lambda i: (0,) * len(shape), pipeline_mode=pl.Buffered(1))


def _mod_spec(tiles_per_cond, first_row):
    return pl.BlockSpec((None, N_MOD, D_MODEL), lambda i: (first_row + i // tiles_per_cond, 0, 0))


def _token_call(kernel, name, n_tokens, mod_spec, in_arrays, in_specs, out_shapes, out_specs):
    del mod_spec
    return pl.pallas_call(
        kernel,
        out_shape=out_shapes,
        grid=(n_tokens // TOKEN_TILE,),
        in_specs=in_specs,
        out_specs=out_specs,
        compiler_params=pltpu.CompilerParams(dimension_semantics=("arbitrary",), vmem_limit_bytes=VMEM_LIMIT),
        name=name,
    )(*in_arrays)


def _row_tile(width):
    return pl.BlockSpec((TOKEN_TILE, width), lambda i: (i, 0))


def _ffn1(x, mod, mod_spec, gain, w_in, w_out, name):
    n = x.shape[0]
    return _token_call(
        _ffn1_kernel, name, n, mod_spec,
        (x, mod, gain, w_in, w_out),
        [_row_tile(D_MODEL), mod_spec, _resident((1, D_MODEL)), _resident(w_in.shape), _resident(w_out.shape)],
        jax.ShapeDtypeStruct((n, D_MODEL), F32), _row_tile(D_MODEL))


def _mix_ffn2(x, o_rec, o_att, mod, mod_spec, w_mo, gain, w_in, w_out, name):
    n = x.shape[0]
    return _token_call(
        _mix_ffn2_kernel, name, n, mod_spec,
        (x, o_rec, o_att, mod, w_mo, gain, w_in, w_out),
        [_row_tile(D_MODEL), _row_tile(MIX_A), _row_tile(MIX_B), mod_spec, _resident(w_mo.shape),
         _resident((1, D_MODEL)), _resident(w_in.shape), _resident(w_out.shape)],
        jax.ShapeDtypeStruct((n, D_MODEL), F32), _row_tile(D_MODEL))


def _group_rms(a, ones_blockdiag, gain):
    ss = _dot((a * a).astype(BF16), ones_blockdiag)
    return a * lax.rsqrt(ss * (1.0 / D_HEAD_B) + EPS) * gain


def _rope(x, cos, sin_signed):
    lanes = cos.shape[-1]
    low = (lax.broadcasted_iota(jnp.int32, (1, lanes), 1) % 32) < 16
    out = []
    for j in range(x.shape[-1] // lanes):
        xb = x[:, j * lanes:(j + 1) * lanes]
        partner = jnp.where(low, pltpu.roll(xb, lanes - 16, 1), pltpu.roll(xb, 16, 1))
        out.append(xb * cos + partner * sin_signed)
    return jnp.concatenate(out, axis=-1)


def _proj_kernel(*refs, rope):
    if rope:
        (x_ref, mod_ref, gain_ref, w_ref, gsum_ref, qg_ref, kg_ref, cos_ref, sin_ref,
         qi_ref, hf_ref, sg_ref, qa_ref, ka_ref, va_ref) = refs
    else:
        (x_ref, mod_ref, gain_ref, w_ref, gsum_ref, qg_ref, kg_ref,
         qi_ref, hf_ref, sg_ref, qa_ref, ka_ref, va_ref) = refs
    hb = _mod_norm(x_ref[...], gain_ref[...], mod_ref[4:5, :], mod_ref[3:4, :]).astype(BF16)

    def piece(p):
        return _dot(hb, w_ref[:, p * MIX_A:(p + 1) * MIX_A])

    qi_ref[:, :MIX_A] = piece(0).astype(qi_ref.dtype)
    qi_ref[:, MIX_A:] = piece(1).astype(qi_ref.dtype)
    hf_ref[:, :MIX_A] = piece(2)
    hf_ref[:, MIX_A:] = piece(3)
    sg_ref[...] = _silu(piece(4)).astype(sg_ref.dtype)
    qn = _group_rms(piece(5), gsum_ref[...], qg_ref[...])
    kn = _group_rms(piece(6), gsum_ref[...], kg_ref[...])
    if rope:
        qn = _rope(qn, cos_ref[...], sin_ref[...])
        kn = _rope(kn, cos_ref[...], sin_ref[...])
    qa_ref[...] = qn.astype(qa_ref.dtype)
    ka_ref[...] = kn.astype(ka_ref.dtype)
    va_ref[...] = piece(7).astype(va_ref.dtype)


def _proj(x, mod, mod_spec, gain, w_mi, gsum, qg, kg, rope_tables, kv_dtype, name):
    n = x.shape[0]
    arrays = [x, mod, gain, w_mi, gsum, qg, kg]
    specs = [_row_tile(D_MODEL), mod_spec, _resident((1, D_MODEL)), _resident(w_mi.shape), _resident(gsum.shape),
             _resident((1, MIX_B)), _resident((1, MIX_B))]
    if rope_tables is not None:
        cos, sin = rope_tables
        tiles_per_seq = cos.shape[0] // TOKEN_TILE
        table_spec = pl.BlockSpec((TOKEN_TILE, cos.shape[1]), lambda i: (i % tiles_per_seq, 0))
        arrays += [cos, sin]
        specs += [table_spec, table_spec]
    out_shapes = (jax.ShapeDtypeStruct((n, 2 * MIX_A), BF16), jax.ShapeDtypeStruct((n, 2 * MIX_A), F32),
                  jax.ShapeDtypeStruct((n, MIX_A), BF16), jax.ShapeDtypeStruct((n, MIX_B), BF16),
                  jax.ShapeDtypeStruct((n, MIX_B), kv_dtype), jax.ShapeDtypeStruct((n, MIX_B), kv_dtype))
    out_specs = (_row_tile(2 * MIX_A), _row_tile(2 * MIX_A), _row_tile(MIX_A), _row_tile(MIX_B),
                 _row_tile(MIX_B), _row_tile(MIX_B))
    return _token_call(functools.partial(_proj_kernel, rope=rope_tables is not None), name, n, mod_spec,
                       arrays, specs, out_shapes, out_specs)


def _chunk_cumsum(x, reverse):
    row = lax.broadcasted_iota(jnp.int32, x.shape, 0)
    k = 1
    while k < CHUNK:
        if reverse:
            x = x + jnp.where(row < CHUNK - k, pltpu.roll(x, CHUNK - k, 0), 0.0)
        else:
            x = x + jnp.where(row >= k, pltpu.roll(x, k, 0), 0.0)
        k *= 2
    return x


def _hgrn_unit(q, v, hf, lb, state_t, ones, reverse):
    f = lb + (1.0 - lb) * jax.nn.sigmoid(hf)
    l2f = jnp.log(f) * LOG2E
    l2k = jnp.log(1.0 - f) * LOG2E
    b = _chunk_cumsum(l2f, reverse)
    c = b - l2k
    groups = CHUNK // SUBLANES
    b_g = [b[g * SUBLANES:(g + 1) * SUBLANES] for g in range(groups)]
    q_g = [q[g * SUBLANES:(g + 1) * SUBLANES] for g in range(groups)]
    row = lax.broadcasted_iota(jnp.int32, (SUBLANES, DK_A), 0)
    lane = lax.broadcasted_iota(jnp.int32, (SUBLANES, DK_A), 1)

    pieces, index = [], []
    for s in range(CHUNK):
        c_s = c[s:s + 1, :]
        gs = s // SUBLANES
        for g in (range(0, gs + 1) if reverse else range(gs, groups)):
            w = q_g[g] * jnp.exp2(b_g[g] - c_s)
            if g == gs:
                keep = (row <= s % SUBLANES) if reverse else (row >= s % SUBLANES)
                w = jnp.where(keep, w, 0.0)
            pieces.append(w)
            index.append((g, s))
    summed = _dot(jnp.concatenate(pieces, axis=0).astype(BF16), ones)
    p_g = [jnp.zeros((SUBLANES, DK_A), F32) for _ in range(groups)]
    for n, (g, s) in enumerate(index):
        p_g[g] = jnp.where(lane == s, summed[n * SUBLANES:(n + 1) * SUBLANES], p_g[g])
    p = jnp.concatenate(p_g, axis=0)[:, :CHUNK]

    state_b = state_t.astype(BF16)
    o = _dot_nt((q * jnp.exp2(b)).astype(BF16), state_b) + _dot(p.astype(BF16), v)
    b_far = b[0:1, :] if reverse else b[CHUNK - 1:CHUNK, :]
    k_far = jnp.exp2(b_far - c).astype(BF16)
    new_state = state_t * jnp.exp2(b_far) + _dot_tn(v, k_far)
    return o, new_state


def _hgrn_kernel(*refs, seq_len, has_init, want_state):
    refs = list(refs)
    qi_ref, hf_ref, sg_ref, lbl_ref, gn_ref = refs[:5]
    pos = 5
    init_ref = None
    if has_init:
        init_ref = refs[pos]
        pos += 1
    o_ref = refs[pos]
    pos += 1
    fin_ref = None
    if want_state:
        fin_ref = refs[pos]
        pos += 1
    of_s, ob_s, st_s = refs[pos:pos + 3]

    logits = lbl_ref[...]
    e = jnp.exp(logits - jnp.max(logits, axis=0, keepdims=True))
    lb = e[0:1, :] / jnp.sum(e, axis=0, keepdims=True)
    ones = jnp.ones((DK_A, DK_A), BF16)
    for u in range(2 * H_A):
        st_s[u] = init_ref[u].T if has_init else jnp.zeros((DV_A, DK_A), F32)

    n_chunks = seq_len // CHUNK

    def body(i, carry):
        for d in range(2):
            r0 = pl.multiple_of((i if d == 0 else n_chunks - 1 - i) * CHUNK, CHUNK)
            out_s = of_s if d == 0 else ob_s
            for h in range(H_A):
                cols = slice(h * DK_A, (h + 1) * DK_A)
                q = qi_ref[pl.ds(r0, CHUNK), cols].astype(F32)
                v = qi_ref[pl.ds(r0, CHUNK), MIX_A + h * DV_A:MIX_A + (h + 1) * DV_A]
                hf = hf_ref[pl.ds(r0, CHUNK), d * MIX_A + h * DK_A:d * MIX_A + (h + 1) * DK_A]
                o, st = _hgrn_unit(q, v, hf, lb[:, cols], st_s[d * H_A + h], ones, reverse=(d == 1))
                out_s[pl.ds(r0, CHUNK), cols] = o
                st_s[d * H_A + h] = st
        return carry

    lax.fori_loop(0, n_chunks, body, 0)

    for h in range(H_A):
        cols = slice(h * DV_A, (h + 1) * DV_A)
        o = of_s[:, cols] + ob_s[:, cols]
        ms = jnp.mean(o * o, axis=-1, keepdims=True)
        o = o * lax.rsqrt(ms + EPS) * gn_ref[...]
        o_ref[:, cols] = (o * sg_ref[:, cols].astype(F32)).astype(o_ref.dtype)
    if want_state:
        for u in range(2 * H_A):
            fin_ref[u] = st_s[u].T


def _hgrn(qi, hf, sg, lb_logits, out_gain, init_state, want_state, name):
    b, t, _ = qi.shape
    arrays = [qi, hf, sg, lb_logits, out_gain]
    specs = [pl.BlockSpec((None, t, 2 * MIX_A), lambda i: (i, 0, 0)),
             pl.BlockSpec((None, t, 2 * MIX_A), lambda i: (i, 0, 0)),
             pl.BlockSpec((None, t, MIX_A), lambda i: (i, 0, 0)),
             pl.BlockSpec(lb_logits.shape, lambda i: (0, 0)),
             pl.BlockSpec((1, DV_A), lambda i: (0, 0))]
    state_spec = pl.BlockSpec((None, 2 * H_A, DK_A, DV_A), lambda i: (i, 0, 0, 0))
    if init_state is not None:
        arrays.append(init_state)
        specs.append(state_spec)
    out_shapes = [jax.ShapeDtypeStruct((b, t, MIX_A), BF16)]
    out_specs = [pl.BlockSpec((None, t, MIX_A), lambda i: (i, 0, 0))]
    if want_state:
        out_shapes.append(jax.ShapeDtypeStruct((b, 2 * H_A, DK_A, DV_A), F32))
        out_specs.append(state_spec)
    return pl.pallas_call(
        functools.partial(_hgrn_kernel, seq_len=t, has_init=init_state is not None, want_state=want_state),
        out_shape=tuple(out_shapes),
        grid=(b,),
        in_specs=specs,
        out_specs=tuple(out_specs),
        scratch_shapes=[pltpu.VMEM((t, MIX_A), F32), pltpu.VMEM((t, MIX_A), F32),
                        pltpu.VMEM((2 * H_A, DV_A, DK_A), F32)],
        compiler_params=pltpu.CompilerParams(dimension_semantics=("arbitrary",), vmem_limit_bytes=VMEM_LIMIT),
        name=name,
    )(*arrays)


def _softmax_rows(s):
    e = jnp.exp(s - jnp.max(s, axis=-1, keepdims=True))
    return e * (1.0 / jnp.sum(e, axis=-1, keepdims=True))


def _attn_kernel(*refs, has_cache):
    if has_cache:
        q_ref, k_ref, v_ref, ck_ref, cv_ref, lam_ref, gain_ref, o_ref = refs
    else:
        q_ref, k_ref, v_ref, lam_ref, gain_ref, o_ref = refs
    lam_p = lam_ref[...]
    lam = (jnp.exp(jnp.sum(lam_p[0:1] * lam_p[1:2], axis=-1, keepdims=True))
           - jnp.exp(jnp.sum(lam_p[2:3] * lam_p[3:4], axis=-1, keepdims=True)) + LAM_INIT)
    width = 2 * D_HEAD_B
    first = lax.broadcasted_iota(jnp.int32, (1, width), 1) < D_HEAD_B
    scale = D_HEAD_B ** -0.5
    for h in range(H_B):
        cols = slice(h * width, (h + 1) * width)
        q = q_ref[:, cols] * jnp.asarray(scale, q_ref.dtype)
        k = k_ref[:, cols].astype(BF16)
        v = v_ref[:, cols].astype(BF16)
        if has_cache:
            k = jnp.concatenate([ck_ref[:, cols].astype(BF16), k], axis=0)
            v = jnp.concatenate([cv_ref[:, cols].astype(BF16), v], axis=0)
        zero = jnp.zeros_like(q)
        p0 = _softmax_rows(_dot_nt(jnp.where(first, q, zero), k))
        p1 = _softmax_rows(_dot_nt(jnp.where(first, zero, q), k))
        o = _dot((p0 - lam * p1).astype(BF16), v)
        ms = jnp.mean(o * o, axis=-1, keepdims=True)
        o_ref[:, cols] = (o * lax.rsqrt(ms + EPS) * gain_ref[...] * (1.0 - LAM_INIT)).astype(o_ref.dtype)


def _attention(q, k, v, cache_k, cache_v, lam_params, subln_gain, name):
    b, t, _ = q.shape
    qb = min(ATTN_Q_BLOCK, t)
    arrays = [q, k, v]
    specs = [pl.BlockSpec((None, qb, MIX_B), lambda i, j: (i, j, 0)),
             pl.BlockSpec((None, t, MIX_B), lambda i, j: (i, 0, 0)),
             pl.BlockSpec((None, t, MIX_B), lambda i, j: (i, 0, 0))]
    if cache_k is not None:
        p = cache_k.shape[1]
        arrays += [cache_k, cache_v]
        specs += [pl.BlockSpec((None, p, MIX_B), lambda i, j: (i, 0, 0))] * 2
    arrays += [lam_params, subln_gain]
    specs += [pl.BlockSpec(lam_params.shape, lambda i, j: (0, 0)), pl.BlockSpec((1, 2 * D_HEAD_B), lambda i, j: (0, 0))]
    return pl.pallas_call(
        functools.partial(_attn_kernel, has_cache=cache_k is not None),
        out_shape=jax.ShapeDtypeStruct((b, t, MIX_B), BF16),
        grid=(b, t // qb),
        in_specs=specs,
        out_specs=pl.BlockSpec((None, qb, MIX_B), lambda i, j: (i, j, 0)),
        compiler_params=pltpu.CompilerParams(dimension_semantics=("arbitrary", "arbitrary"),
                                             vmem_limit_bytes=VMEM_LIMIT),
        name=name,
    )(*arrays)


def _rope_lane_tables(n_tokens):
    pos = jnp.arange(n_tokens)
    pos_row = (pos // GRID_W).astype(F32)
    pos_col = (pos % GRID_W).astype(F32)
    half = D_HEAD_B // 2
    inv = ROPE_THETA ** (-jnp.arange(0, half, 2, dtype=F32) / half)
    ang_r = pos_row[:, None] * inv
    ang_c = pos_col[:, None] * inv
    cos = jnp.concatenate([jnp.cos(ang_r), jnp.cos(ang_r), jnp.cos(ang_c), jnp.cos(ang_c)], axis=-1)
    sin = jnp.concatenate([-jnp.sin(ang_r), jnp.sin(ang_r), -jnp.sin(ang_c), jnp.sin(ang_c)], axis=-1)
    return jnp.tile(cos, (1, 2)), jnp.tile(sin, (1, 2))


def _layer_path(x, mod, mod_spec, p, seq_shape, rope_tables, cache, init_state, want_state, tag):
    b, t = seq_shape
    kv_dtype = F32 if want_state else BF16
    x1 = _ffn1(x, mod, mod_spec, p["norm_ffn1"], p["w_ffn1_in"], p["w_ffn1_out"], "ffn1_" + tag)
    qi, hf, sg, qa, ka, va = _proj(x1, mod, mod_spec, p["norm_mix"], p["w_mix_in"], p["gsum"], p["q_gain"],
                                   p["k_gain"], rope_tables, kv_dtype, "mix_in_" + tag)
    seq = lambda a: a.reshape(b, t, a.shape[-1])
    hgrn_out = _hgrn(seq(qi), seq(hf), seq(sg), p["lb_logits"], p["hgrn_out_norm"], init_state, want_state,
                     "hgrn2_" + tag)
    o_rec = hgrn_out[0]
    states = hgrn_out[1] if want_state else None
    ck, cv = cache if cache is not None else (None, None)
    o_att = _attention(seq(qa), seq(ka), seq(va), ck, cv, p["attn_lambda"], p["attn_subln"], "diff_attn_" + tag)
    y = _mix_ffn2(x1, o_rec.reshape(b * t, MIX_A), o_att.reshape(b * t, MIX_B), mod, mod_spec, p["w_mix_out"],
                  p["norm_ffn2"], p["w_ffn2_in"], p["w_ffn2_out"], "mix_out_ffn2_" + tag)
    return y, ka, va, states


def kernel(x_prompt, x_sample, c, cache_attn_k, cache_attn_v, state_hgrn, c_ctx, w_ada, b_ada, norm_ffn1, w_ffn1_in, w_ffn1_out, norm_mix, w_mix_in, w_mix_out, hgrn_lb_logits, hgrn_out_norm, attn_q_norm, attn_k_norm, attn_lambda, attn_subln, norm_ffn2, w_ffn2_in, w_ffn2_out):
    assert w_ada.shape[0] == 1, "single-layer step"
    batch, seq, _ = x_prompt.shape
    dec_batch, dec_seq, _ = x_sample.shape
    past = cache_attn_k.shape[2]

    cond = jnp.zeros((SUBLANES, D_MODEL), F32).at[0].set(c_ctx).at[1:1 + dec_batch].set(c)
    mod = _ada_modulation(cond, w_ada[0], b_ada[0])

    group = jnp.arange(MIX_B) // D_HEAD_B
    p = {
        "norm_ffn1": norm_ffn1[0].reshape(1, D_MODEL), "w_ffn1_in": w_ffn1_in[0].astype(BF16),
        "w_ffn1_out": w_ffn1_out[0].astype(BF16),
        "norm_mix": norm_mix[0].reshape(1, D_MODEL), "w_mix_in": w_mix_in[0].astype(BF16),
        "w_mix_out": w_mix_out[0].astype(BF16),
        "norm_ffn2": norm_ffn2[0].reshape(1, D_MODEL), "w_ffn2_in": w_ffn2_in[0].astype(BF16),
        "w_ffn2_out": w_ffn2_out[0].astype(BF16),
        "gsum": (group[:, None] == group[None, :]).astype(BF16),
        "q_gain": jnp.tile(attn_q_norm[0], MIX_B // D_HEAD_B).reshape(1, MIX_B),
        "k_gain": jnp.tile(attn_k_norm[0], MIX_B // D_HEAD_B).reshape(1, MIX_B),
        "lb_logits": hgrn_lb_logits, "hgrn_out_norm": hgrn_out_norm[0].reshape(1, DV_A),
        "attn_lambda": attn_lambda[0], "attn_subln": attn_subln[0].reshape(1, 2 * D_HEAD_B),
    }

    y_prompt, k_ctx, v_ctx, s_ctx = _layer_path(
        x_prompt.reshape(batch * seq, D_MODEL), mod, _mod_spec(batch * seq // TOKEN_TILE, 0), p, (batch, seq),
        None, None, None, True, "ctx")

    cache = (cache_attn_k[:, 0].reshape(dec_batch, past, MIX_B), cache_attn_v[:, 0].reshape(dec_batch, past, MIX_B))
    init_state = state_hgrn[:, 0].reshape(dec_batch, 2 * H_A, DK_A, DV_A)
    y_sample, _, _, _ = _layer_path(
        x_sample.reshape(dec_batch * dec_seq, D_MODEL), mod, _mod_spec(dec_seq // TOKEN_TILE, 1), p,
        (dec_batch, dec_seq), _rope_lane_tables(dec_seq), cache, init_state, False, "lat")

    return (y_prompt.reshape(batch, seq, D_MODEL),
            y_sample.reshape(dec_batch, dec_seq, D_MODEL),
            k_ctx.reshape(batch, 1, seq, H_B, 2, D_HEAD_B),
            v_ctx.reshape(batch, 1, seq, H_B, 2 * D_HEAD_B),
            s_ctx.reshape(batch, 1, 2, H_A, DK_A, DV_A))
```
